```python
import jax, jax.numpy as jnp
from jax import lax
import numpy as np

D_MODEL = 1024
BATCH = 4
SEQ = 4096
DEPTH = 1

D_MIX = D_MODEL
ATTN_WIDTH = D_MIX // 2
HGRN_WIDTH = D_MIX - ATTN_WIDTH
ATTN_HEAD_DIM = 64
ATTN_HEADS = ATTN_WIDTH // ATTN_HEAD_DIM
DILATED_BRANCHES = ((128, 1), (512, 4), (2048, 16))
ROPE_THETA = 500000.0
ROPE_DIM = ATTN_HEAD_DIM // 4
HGRN_EXPAND = 128
HGRN_HEADS = HGRN_WIDTH // HGRN_EXPAND
HGRN_CHUNK = 64
IN_COLS = 3 * ATTN_WIDTH + 4 * HGRN_WIDTH
N_EXPERTS = 32
TOP_K = 4
D_EXPERT = D_MODEL
SWIGLU_LIMIT = 7.0
SWIGLU_ALPHA = 1.702
PLE_DIM = 256
RMS_EPS = 1e-6

kernel_name = "hymba_dilated_hgrn2_moe_ple"


def rms_norm(x, g):
    xf = x.astype(jnp.float32)
    y = xf * lax.rsqrt(jnp.mean(xf * xf, axis=-1, keepdims=True) + RMS_EPS)
    return (y * g.astype(jnp.float32)).astype(x.dtype)


def partial_rope(x, positions):
    half = ROPE_DIM // 2
    inv_freq = ROPE_THETA ** (-jnp.arange(half, dtype=jnp.float32) / half)
    ang = positions.astype(jnp.float32)[..., None] * inv_freq
    cos = jnp.cos(ang)[:, :, None, :]
    sin = jnp.sin(ang)[:, :, None, :]
    xr = x[..., :ROPE_DIM].astype(jnp.float32)
    x1, x2 = xr[..., :half], xr[..., half:]
    rot = jnp.concatenate([x1 * cos - x2 * sin, x2 * cos + x1 * sin], axis=-1).astype(x.dtype)
    return jnp.concatenate([rot, x[..., ROPE_DIM:]], axis=-1)


def dilated_branch(q, k, v, window, dilation):
    B, S, H, hd = q.shape
    L = S // dilation
    W = window // dilation
    nblk = -(-L // W)
    Lp = nblk * W

    def to_blocks(t):
        t = t.reshape(B, L, dilation, H, hd).transpose(0, 2, 3, 1, 4)
        t = jnp.pad(t, ((0, 0), (0, 0), (0, 0), (0, Lp - L), (0, 0)))
        return t.reshape(B, dilation, H, nblk, W, hd)

    qb, kb, vb = to_blocks(q), to_blocks(k), to_blocks(v)
    pad_prev = ((0, 0), (0, 0), (0, 0), (1, 0), (0, 0), (0, 0))
    kk = jnp.concatenate([jnp.pad(kb, pad_prev)[:, :, :, :-1], kb], axis=4)
    vv = jnp.concatenate([jnp.pad(vb, pad_prev)[:, :, :, :-1], vb], axis=4)
    scores = jnp.einsum('bdhnqc,bdhnkc->bdhnqk', qb, kk).astype(jnp.float32)
    blk = jnp.arange(nblk)[:, None] * W
    qpos = blk + jnp.arange(W)[None, :]
    kpos = blk - W + jnp.arange(2 * W)[None, :]
    rel = qpos[:, :, None] - kpos[:, None, :]
    mask = (rel >= 0) & (rel <= W) & (kpos[:, None, :] >= 0)
    scores = jnp.where(mask, scores, -jnp.inf)
    lse = jax.nn.logsumexp(scores, axis=-1)
    probs = jnp.exp(scores - lse[..., None]).astype(v.dtype)
    out = jnp.einsum('bdhnqk,bdhnkc->bdhnqc', probs, vv)
    out = out.reshape(B, dilation, H, Lp, hd)[:, :, :, :L]
    out = out.transpose(0, 3, 1, 2, 4).reshape(B, S, H, hd)
    lse = lse.reshape(B, dilation, H, Lp)[:, :, :, :L].transpose(0, 3, 1, 2).reshape(B, S, H)
    return out, lse


def dilated_attention(q, k, v):
    outs, lses = [], []
    for window, dilation in DILATED_BRANCHES:
        o, l = dilated_branch(q, k, v, window, dilation)
        outs.append(o)
        lses.append(l)
    w = jax.nn.softmax(jnp.stack(lses, axis=0), axis=0)
    out = jnp.sum(w[..., None] * jnp.stack(outs, axis=0).astype(jnp.float32), axis=0)
    return out.astype(q.dtype)


def hgrn2_mixer(q_raw, f_raw, i_raw, g_raw, lower_bound, norm_g):
    B, S, _ = q_raw.shape
    H, E, C = HGRN_HEADS, HGRN_EXPAND, HGRN_CHUNK
    N = S // C
    f32 = jnp.float32
    q = jax.nn.silu(q_raw.astype(f32))
    lb = lower_bound.astype(f32)
    f = lb + (1.0 - lb) * jax.nn.sigmoid(f_raw.astype(f32))
    k = 1.0 - f
    log_f = jnp.log(f)
    v = i_raw.astype(f32)

    def to_chunks(t):
        return t.reshape(B, N, C, H, E).transpose(1, 0, 3, 2, 4)

    causal = jnp.tril(jnp.ones((C, C), dtype=bool))[:, :, None]

    def chunk_step(state, inp):
        qc, kc, vc, gc = inp
        b = jnp.cumsum(gc, axis=2)
        o_inter = jnp.einsum('bhtc,bhcv->bhtv', qc * jnp.exp(b), state)
        diff = b[:, :, :, None, :] - b[:, :, None, :, :]
        decay = jnp.exp(jnp.where(causal, diff, -jnp.inf))
        A = jnp.einsum('bhtc,bhsc,bhtsc->bhts', qc, kc, decay)
        o = o_inter + jnp.einsum('bhts,bhsv->bhtv', A, vc)
        b_last = b[:, :, -1, :]
        new_state = jnp.exp(b_last)[..., None] * state + jnp.einsum(
            'bhsc,bhsv->bhcv', kc * jnp.exp(b_last[:, :, None, :] - b), vc)
        return new_state, o

    state0 = jnp.zeros((B, H, E, E), f32)
    _, o = lax.scan(chunk_step, state0, (to_chunks(q), to_chunks(k), to_chunks(v), to_chunks(log_f)))
    o = o.transpose(1, 0, 3, 2, 4).reshape(B, S, H, E)
    gate = g_raw.astype(f32).reshape(B, S, H, E)
    o = rms_norm(o, norm_g) * jax.nn.silu(gate)
    return o.reshape(B, S, H * E).astype(q_raw.dtype)


def moe_ffn(x, router_w, router_b, w_gu, b_gu, w_down, b_down):
    B, S, D = x.shape
    t = x.reshape(B * S, D)
    logits = (t @ router_w + router_b).astype(jnp.float32)
    top_val, top_idx = lax.top_k(logits, TOP_K)
    top_w = jax.nn.softmax(top_val, axis=-1)
    gates = jnp.sum(jax.nn.one_hot(top_idx, N_EXPERTS, dtype=jnp.float32) * top_w[..., None], axis=1)
    gates = gates.astype(x.dtype)
    out = jnp.zeros_like(t)
    for e in range(N_EXPERTS):
        gu = t @ w_gu[e] + b_gu[e]
        gate = jnp.minimum(gu[:, :D_EXPERT], SWIGLU_LIMIT)
        up = jnp.clip(gu[:, D_EXPERT:], -SWIGLU_LIMIT, SWIGLU_LIMIT)
        glu = gate * jax.nn.sigmoid(SWIGLU_ALPHA * gate)
        y = ((up + 1.0) * glu) @ w_down[e] + b_down[e]
        out = out + gates[:, e:e + 1] * y
    return out.reshape(B, S, D)


def setup_inputs(seed: int = 0) -> dict:
    key = jax.random.key(seed)
    ks = jax.random.split(key, 20)
    f32 = jnp.float32
    nrm = lambda k, shape, scale: jax.random.normal(k, shape, f32) * scale
    return {
        "x": nrm(ks[0], (BATCH, SEQ, D_MODEL), 1.0),
        "p": nrm(ks[1], (DEPTH, BATCH, SEQ, PLE_DIM), 1.0),
        "positions": jnp.broadcast_to(jnp.arange(SEQ, dtype=jnp.int32), (BATCH, SEQ)),
        "mix_norm_g": 1.0 + nrm(ks[2], (DEPTH, D_MODEL), 0.02),
        "w_in": nrm(ks[3], (DEPTH, D_MODEL, IN_COLS), D_MODEL ** -0.5),
        "q_norm_g": 1.0 + nrm(ks[4], (DEPTH, ATTN_HEAD_DIM), 0.02),
        "k_norm_g": 1.0 + nrm(ks[5], (DEPTH, ATTN_HEAD_DIM), 0.02),
        "hgrn_lb_logits": nrm(ks[6], (DEPTH + 1, HGRN_WIDTH), 0.5),
        "hgrn_norm_g": 1.0 + nrm(ks[7], (DEPTH, HGRN_EXPAND), 0.02),
        "w_out": nrm(ks[8], (DEPTH, D_MIX, D_MODEL), D_MIX ** -0.5),
        "ffn_norm_g": 1.0 + nrm(ks[9], (DEPTH, D_MODEL), 0.02),
        "router_w": nrm(ks[10], (DEPTH, D_MODEL, N_EXPERTS), D_MODEL ** -0.5),
        "router_b": nrm(ks[11], (DEPTH, N_EXPERTS), 0.01),
        "expert_w_gate_up": nrm(ks[12], (DEPTH, N_EXPERTS, D_MODEL, 2 * D_EXPERT), D_MODEL ** -0.5),
        "expert_b_gate_up": nrm(ks[13], (DEPTH, N_EXPERTS, 2 * D_EXPERT), 0.02),
        "expert_w_down": nrm(ks[14], (DEPTH, N_EXPERTS, D_EXPERT, D_MODEL), D_EXPERT ** -0.5),
        "expert_b_down": nrm(ks[15], (DEPTH, N_EXPERTS, D_MODEL), 0.02),
        "ple_proj": nrm(ks[16], (DEPTH, PLE_DIM, D_MODEL), PLE_DIM ** -0.5),
        "ple_gate": nrm(ks[17], (DEPTH, D_MODEL, D_MODEL), D_MODEL ** -0.5),
    }


def reference(x, p, positions, mix_norm_g, w_in, q_norm_g, k_norm_g, hgrn_lb_logits,
              hgrn_norm_g, w_out, ffn_norm_g, router_w, router_b, expert_w_gate_up,
              expert_b_gate_up, expert_w_down, expert_b_down, ple_proj, ple_gate):
    B, S, _ = x.shape
    lower_bounds = jnp.cumsum(jax.nn.softmax(hgrn_lb_logits.astype(jnp.float32), axis=0), axis=0)
    splits = [ATTN_WIDTH * j for j in (1, 2, 3)] + [3 * ATTN_WIDTH + HGRN_WIDTH * j for j in (1, 2, 3)]
    h = x
    for i in range(DEPTH):
        a = rms_norm(h, mix_norm_g[i])
        z = a @ w_in[i]
        q_a, k_a, v_a, q_h, f_h, i_h, g_h = jnp.split(z, splits, axis=-1)
        q = q_a.reshape(B, S, ATTN_HEADS, ATTN_HEAD_DIM)
        k = k_a.reshape(B, S, ATTN_HEADS, ATTN_HEAD_DIM)
        v = v_a.reshape(B, S, ATTN_HEADS, ATTN_HEAD_DIM)
        q = partial_rope(rms_norm(q, q_norm_g[i]), positions) * (ATTN_HEAD_DIM ** -0.5)
        k = partial_rope(rms_norm(k, k_norm_g[i]), positions)
        attn = dilated_attention(q, k, v).reshape(B, S, ATTN_WIDTH)
        rec = hgrn2_mixer(q_h, f_h, i_h, g_h, lower_bounds[i], hgrn_norm_g[i])
        h = h + jnp.concatenate([attn, rec], axis=-1) @ w_out[i]
        m = rms_norm(h, ffn_norm_g[i])
        h = h + moe_ffn(m, router_w[i], router_b[i], expert_w_gate_up[i], expert_b_gate_up[i],
                        expert_w_down[i], expert_b_down[i])
        h = h + (p[i] @ ple_proj[i]) * jax.nn.sigmoid(h @ ple_gate[i])
    return h
```

```python
import functools
import math

import numpy as np
import jax
import jax.numpy as jnp
from jax import lax
from jax.experimental import pallas as pl
from jax.experimental.pallas import tpu as pltpu

F32 = jnp.float32
BF16 = jnp.bfloat16

ATTN_WIDTH = 512
HGRN_WIDTH = 512
HEAD_DIM = 64
DILATED_BRANCHES = ((128, 1), (512, 4), (2048, 16))
ATTN_BLOCK = 128
ROPE_THETA = 500000.0
ROPE_DIM = HEAD_DIM // 4
HGRN_EXPAND = 128
HGRN_HEADS = HGRN_WIDTH // HGRN_EXPAND
N_EXPERTS = 32
TOP_K = 4
SWIGLU_LIMIT = 7.0
SWIGLU_ALPHA = 1.702
RMS_EPS = 1e-6
NEG_BIG = -1e30

LANES = 128
VMEM_LIMIT = 56 * 1024 * 1024

TM_PROJ = 512
HGRN_CHUNK = 128
HGRN_SEQ_BLOCK = 1024
TM_EXPERT = 512
TM_COMBINE = 256


def _split3(x):
    hi = x.astype(BF16)
    r1 = x - hi.astype(F32)
    mid = r1.astype(BF16)
    lo = (r1 - mid.astype(F32)).astype(BF16)
    return hi, mid, lo


def _dot(a, b):
    return jnp.dot(a, b, preferred_element_type=F32)


def _dot_nt(a, b):
    return lax.dot_general(a, b, (((1,), (1,)), ((), ())), preferred_element_type=F32)


def _dot_tn(a, b):
    return lax.dot_general(a, b, (((0,), (0,)), ((), ())), preferred_element_type=F32)


def _sigmoid(x):
    return 1.0 / (1.0 + jnp.exp(-x))


def _in_proj_kernel(x_ref, g_ref, w_ref, seg_ref, qg_ref, kg_ref, cos_ref, sa_ref, sb_ref,
                    q_out, k_out, v_out, zh_out):
    x = x_ref[...]
    ms = jnp.mean(x * x, axis=-1, keepdims=True)
    a = (x * lax.rsqrt(ms + RMS_EPS) * g_ref[...]).astype(BF16)
    z = _dot(a, w_ref[...])

    reps = ATTN_WIDTH // LANES
    cos = jnp.concatenate([cos_ref[...]] * reps, axis=1)
    sa = jnp.concatenate([sa_ref[...]] * reps, axis=1)
    sb = jnp.concatenate([sb_ref[...]] * reps, axis=1)
    seg = seg_ref[...]

    def qk_norm_rope(t, g):
        sq = t * t
        hi, mid, lo = _split3(sq)
        ss = _dot(hi, seg) + _dot(mid, seg) + _dot(lo, seg)
        y = t * lax.rsqrt(ss * (1.0 / HEAD_DIM) + RMS_EPS) * g
        return (y * cos + pltpu.roll(y, ROPE_DIM // 2, 1) * sa
                + pltpu.roll(y, ATTN_WIDTH - ROPE_DIM // 2, 1) * sb)

    q_out[...] = qk_norm_rope(z[:, 0:ATTN_WIDTH], qg_ref[...]) * (HEAD_DIM ** -0.5)
    k_out[...] = qk_norm_rope(z[:, ATTN_WIDTH:2 * ATTN_WIDTH], kg_ref[...])
    v_out[...] = z[:, 2 * ATTN_WIDTH:3 * ATTN_WIDTH]
    zh_out[...] = z[:, 3 * ATTN_WIDTH:]


def _rope_tables(positions):
    half = ROPE_DIM // 2
    inv_freq = ROPE_THETA ** (-jnp.arange(half, dtype=F32) / half)
    ang = positions.astype(F32).reshape(-1)[:, None] * inv_freq
    cos, sin = jnp.cos(ang), jnp.sin(ang)
    t = ang.shape[0]
    ones = jnp.ones((t, HEAD_DIM - ROPE_DIM), F32)
    zeros_h = jnp.zeros((t, half), F32)
    zeros_r = jnp.zeros((t, HEAD_DIM - ROPE_DIM), F32)
    cos_h = jnp.concatenate([cos, cos, ones], axis=1)
    sa_h = jnp.concatenate([zeros_h, sin, zeros_r], axis=1)
    sb_h = jnp.concatenate([-sin, zeros_h, zeros_r], axis=1)
    two = lambda a: jnp.concatenate([a, a], axis=1)
    return two(cos_h), two(sa_h), two(sb_h)


def _in_proj(x2, g, w_bf, qg, kg, cos_t, sa_t, sb_t):
    t, d = x2.shape
    n_cols = w_bf.shape[1]
    hcols = n_cols - 3 * ATTN_WIDTH
    tm = TM_PROJ
    seg = (np.arange(ATTN_WIDTH)[:, None] // HEAD_DIM == np.arange(ATTN_WIDTH)[None, :] // HEAD_DIM)
    seg = jnp.asarray(seg, BF16)
    row = lambda i: (i, 0)
    const = lambda i: (0, 0)
    return pl.pallas_call(
        _in_proj_kernel,
        grid=(t // tm,),
        in_specs=[
            pl.BlockSpec((tm, d), row),
            pl.BlockSpec((1, d), const),
            pl.BlockSpec((d, n_cols), const),
            pl.BlockSpec((ATTN_WIDTH, ATTN_WIDTH), const),
            pl.BlockSpec((1, ATTN_WIDTH), const),
            pl.BlockSpec((1, ATTN_WIDTH), const),
            pl.BlockSpec((tm, LANES), row),
            pl.BlockSpec((tm, LANES), row),
            pl.BlockSpec((tm, LANES), row),
        ],
        out_specs=[
            pl.BlockSpec((tm, ATTN_WIDTH), row),
            pl.BlockSpec((tm, ATTN_WIDTH), row),
            pl.BlockSpec((tm, ATTN_WIDTH), row),
            pl.BlockSpec((tm, hcols), row),
        ],
        out_shape=[
            jax.ShapeDtypeStruct((t, ATTN_WIDTH), F32),
            jax.ShapeDtypeStruct((t, ATTN_WIDTH), F32),
            jax.ShapeDtypeStruct((t, ATTN_WIDTH), F32),
            jax.ShapeDtypeStruct((t, hcols), F32),
        ],
        compiler_params=pltpu.CompilerParams(
            dimension_semantics=("arbitrary",), vmem_limit_bytes=VMEM_LIMIT),
        name="in_proj",
    )(x2, g, w_bf, seg, qg, kg, cos_t, sa_t, sb_t)


def _attn_kernel(q_ref, k_ref, v_ref, o_ref, on_scr, lse_scr):
    seq = q_ref.shape[1]
    blk = ATTN_BLOCK
    lane = lax.broadcasted_iota(jnp.int32, (blk, LANES), 1)
    head0 = lane < HEAD_DIM
    qi = lax.broadcasted_iota(jnp.int32, (blk, 2 * blk), 0)
    kj = lax.broadcasted_iota(jnp.int32, (blk, 2 * blk), 1)
    band = jnp.logical_and(kj >= qi, kj <= qi + blk)

    for bi, (_, dil) in enumerate(DILATED_BRANCHES):
        nblk = seq // dil // blk

        def body(idx, carry, bi=bi, dil=dil, nblk=nblk):
            r = idx // nblk
            n = idx - r * nblk
            start = r + n * (blk * dil)
            pstart = jnp.maximum(start - blk * dil, r)
            if dil > 1:
                cur = pl.ds(start, blk, stride=dil)
                prev = pl.ds(pstart, blk, stride=dil)
            else:
                cur = pl.ds(pl.multiple_of(start, blk), blk)
                prev = pl.ds(pl.multiple_of(pstart, blk), blk)
            q = q_ref[0, cur, :]
            kk = jnp.concatenate([k_ref[0, prev, :], k_ref[0, cur, :]], axis=0).astype(BF16)
            vv = jnp.concatenate([v_ref[0, prev, :], v_ref[0, cur, :]], axis=0).astype(BF16)
            mask = jnp.logical_and(band, kj >= jnp.where(n == 0, blk, 0))
            outs = []
            for h in range(2):
                hmask = head0 if h == 0 else jnp.logical_not(head0)
                qh = jnp.where(hmask, q, 0.0).astype(BF16)
                s = jnp.where(mask, _dot_nt(qh, kk), NEG_BIG)
                m = jnp.max(s, axis=1, keepdims=True)
                p = jnp.exp(s - m)
                l = jnp.sum(p, axis=1, keepdims=True)
                o = _dot(p.astype(BF16), vv)
                outs.append((o / l, m + jnp.log(l)))
            on_scr[bi, cur, :] = jnp.where(head0, outs[0][0], outs[1][0])
            lse_scr[bi, cur, :] = jnp.where(head0, outs[0][1], outs[1][1])
            return carry

        lax.fori_loop(0, dil * nblk, body, 0)

    rows = 512

    def merge(i, carry):
        sl = pl.ds(pl.multiple_of(i * rows, rows), rows)
        l0, l1, l2 = lse_scr[0, sl, :], lse_scr[1, sl, :], lse_scr[2, sl, :]
        mx = jnp.maximum(jnp.maximum(l0, l1), l2)
        w0, w1, w2 = jnp.exp(l0 - mx), jnp.exp(l1 - mx), jnp.exp(l2 - mx)
        num = w0 * on_scr[0, sl, :] + w1 * on_scr[1, sl, :] + w2 * on_scr[2, sl, :]
        o_ref[0, sl, :] = num / (w0 + w1 + w2)
        return carry

    lax.fori_loop(0, seq // rows, merge, 0)


def _attention(q3, k3, v3):
    b, s, w = q3.shape
    blk = (1, s, LANES)
    spec = pl.BlockSpec(blk, lambda i, j: (i, 0, j))
    return pl.pallas_call(
        _attn_kernel,
        grid=(b, w // LANES),
        in_specs=[spec, spec, spec],
        out_specs=spec,
        out_shape=jax.ShapeDtypeStruct((b, s, w), F32),
        scratch_shapes=[pltpu.VMEM((3, s, LANES), F32), pltpu.VMEM((3, s, LANES), F32)],
        compiler_params=pltpu.CompilerParams(
            dimension_semantics=("arbitrary", "arbitrary"), vmem_limit_bytes=VMEM_LIMIT),
        name="attention",
    )(q3, k3, v3)


def _hgrn_consts(c):
    nl = int(math.log2(c))
    t = np.arange(c)[:, None]
    u = np.arange(c)[None, :]
    blocks, masks = [], [np.eye(c, dtype=np.float32)]
    for lvl in range(nl):
        m = 1 << lvl
        bd = (t // (2 * m)) * (2 * m) + m - 1
        qside = (t % (2 * m)) >= m
        blocks.append(np.where(qside, (u > bd) & (u <= t), (u > t) & (u <= bd)))
        masks.append((t // (2 * m)) == (u // (2 * m)))
    blocks.append(u <= t)
    blocks.append(u > t)
    mstack = np.concatenate(blocks, axis=0).astype(np.float32)
    return nl, mstack, np.stack(masks).astype(np.float32)


def _hgrn_kernel(zh_ref, lbl_ref, ng_ref, ms_ref, mk_ref, o_ref, st_ref, *, chunk, nl):
    c = chunk
    e = HGRN_EXPAND

    @pl.when(pl.program_id(1) == 0)
    def _():
        st_ref[...] = jnp.zeros_like(st_ref)

    l0, l1 = lbl_ref[0:1, :], lbl_ref[1:2, :]
    mx = jnp.maximum(l0, l1)
    e0, e1 = jnp.exp(l0 - mx), jnp.exp(l1 - mx)
    lb_all = e0 / (e0 + e1)
    rowi = lax.broadcasted_iota(jnp.int32, (c, e), 0)

    def chunk_body(ci, carry):
        mstack = ms_ref[...]
        rows = pl.ds(pl.multiple_of(ci * c, c), c)
        for h in range(HGRN_HEADS):
            col = lambda j: pl.ds(j * HGRN_WIDTH + h * e, e)
            qr, fr = zh_ref[0, rows, col(0)], zh_ref[0, rows, col(1)]
            v, gr = zh_ref[0, rows, col(2)], zh_ref[0, rows, col(3)]
            lb = lb_all[:, h * e:(h + 1) * e]
            q = qr * _sigmoid(qr)
            f = lb + (1.0 - lb) * _sigmoid(fr)
            k = 1.0 - f
            hi, mid, lo = _split3(jnp.log(f))
            dec = jnp.exp(_dot(mstack, hi) + _dot(mstack, mid) + _dot(mstack, lo))
            vb = v.astype(BF16)
            a = _dot_nt(q.astype(BF16), k.astype(BF16)) * mk_ref[0]
            for lvl in range(nl):
                dl = dec[lvl * c:(lvl + 1) * c]
                qside = ((rowi >> lvl) & 1) == 1
                qt = jnp.where(qside, q * dl, 0.0).astype(BF16)
                kt = jnp.where(qside, 0.0, k * dl).astype(BF16)
                al = _dot_nt(qt, kt)
                a = a + (al if lvl == nl - 1 else al * mk_ref[lvl + 1])
            d_in = dec[nl * c:(nl + 1) * c]
            d_out = dec[(nl + 1) * c:(nl + 2) * c]
            st = st_ref[h]
            o = _dot(a.astype(BF16), vb) + _dot_nt((q * d_in).astype(BF16), st.astype(BF16))
            st_ref[h] = st * d_in[c - 1:c, :] + _dot_tn(vb, (k * d_out).astype(BF16))
            ms = jnp.mean(o * o, axis=-1, keepdims=True)
            y = o * lax.rsqrt(ms + RMS_EPS) * ng_ref[...]
            o_ref[0, rows, pl.ds(h * e, e)] = y * (gr * _sigmoid(gr))
        return carry

    lax.fori_loop(0, zh_ref.shape[1] // c, chunk_body, 0)


def _hgrn(zh3, lb_logits, norm_g):
    b, s, w4 = zh3.shape
    c = HGRN_CHUNK
    sb = min(HGRN_SEQ_BLOCK, s)
    nl, mstack, masks = _hgrn_consts(c)
    kern = functools.partial(_hgrn_kernel, chunk=c, nl=nl)
    return pl.pallas_call(
        kern,
        grid=(b, s // sb),
        in_specs=[
            pl.BlockSpec((1, sb, w4), lambda i, j: (i, j, 0)),
            pl.BlockSpec(lb_logits.shape, lambda i, j: (0, 0)),
            pl.BlockSpec((1, HGRN_EXPAND), lambda i, j: (0, 0)),
            pl.BlockSpec(mstack.shape, lambda i, j: (0, 0)),
            pl.BlockSpec(masks.shape, lambda i, j: (0, 0, 0)),
        ],
        out_specs=pl.BlockSpec((1, sb, HGRN_WIDTH), lambda i, j: (i, j, 0)),
        out_shape=jax.ShapeDtypeStruct((b, s, HGRN_WIDTH), F32),
        scratch_shapes=[pltpu.VMEM((HGRN_HEADS, HGRN_EXPAND, HGRN_EXPAND), F32)],
        compiler_params=pltpu.CompilerParams(
            dimension_semantics=("arbitrary", "arbitrary"), vmem_limit_bytes=VMEM_LIMIT),
        name="hgrn",
    )(zh3, lb_logits, norm_g, jnp.asarray(mstack, BF16), jnp.asarray(masks, F32))


def _out_router_kernel(attn_ref, rec_ref, x_ref, wo_ref, g_ref, rw_ref, rb_ref,
                       h_out, m_out, idx_out, wt_out):
    wo = wo_ref[...]
    h = (x_ref[...] + _dot(attn_ref[...].astype(BF16), wo[:ATTN_WIDTH])
         + _dot(rec_ref[...].astype(BF16), wo[ATTN_WIDTH:]))
    h_out[...] = h
    ms = jnp.mean(h * h, axis=-1, keepdims=True)
    m = h * lax.rsqrt(ms + RMS_EPS) * g_ref[...]
    m_out[...] = m

    m_hi, m_mid, m_lo = _split3(m)
    r_hi, r_mid, r_lo = rw_ref[0], rw_ref[1], rw_ref[2]
    logits = (_dot(m_hi, r_hi) + (_dot(m_hi, r_mid) + _dot(m_mid, r_hi))
              + (_dot(m_hi, r_lo) + _dot(m_mid, r_mid) + _dot(m_lo, r_hi))) + rb_ref[...]

    tm, ne = logits.shape
    eidx = lax.broadcasted_iota(jnp.int32, (tm, ne), 1)
    lane = lax.broadcasted_iota(jnp.int32, (tm, LANES), 1)
    vals = logits
    top_v, idx_acc = [], jnp.zeros((tm, LANES), jnp.int32)
    for kk in range(TOP_K):
        mx = jnp.max(vals, axis=1, keepdims=True)
        ix = jnp.min(jnp.where(vals == mx, eidx, ne), axis=1, keepdims=True)
        vals = jnp.where(eidx == ix, -jnp.inf, vals)
        top_v.append(mx)
        idx_acc = jnp.where(lane == kk, ix, idx_acc)
    ex = [jnp.exp(tv - top_v[0]) for tv in top_v]
    den = ex[0] + ex[1] + ex[2] + ex[3]
    wt_acc = jnp.zeros((tm, LANES), F32)
    for kk in range(TOP_K):
        wt_acc = jnp.where(lane == kk, ex[kk] / den, wt_acc)
    idx_out[...] = idx_acc
    wt_out[...] = wt_acc


def _out_router(attn2, rec2, x2, wo_bf, g, rw3, rb):
    t, d = x2.shape
    tm = TM_PROJ
    row = lambda i: (i, 0)
    const = lambda i: (0, 0)
    return pl.pallas_call(
        _out_router_kernel,
        grid=(t // tm,),
        in_specs=[
            pl.BlockSpec((tm, ATTN_WIDTH), row),
            pl.BlockSpec((tm, HGRN_WIDTH), row),
            pl.BlockSpec((tm, d), row),
            pl.BlockSpec(wo_bf.shape, const),
            pl.BlockSpec((1, d), const),
            pl.BlockSpec(rw3.shape, lambda i: (0, 0, 0)),
            pl.BlockSpec((1, N_EXPERTS), const),
        ],
        out_specs=[
            pl.BlockSpec((tm, d), row),
            pl.BlockSpec((tm, d), row),
            pl.BlockSpec((tm, LANES), row),
            pl.BlockSpec((tm, LANES), row),
        ],
        out_shape=[
            jax.ShapeDtypeStruct((t, d), F32),
            jax.ShapeDtypeStruct((t, d), F32),
            jax.ShapeDtypeStruct((t, LANES), jnp.int32),
            jax.ShapeDtypeStruct((t, LANES), F32),
        ],
        compiler_params=pltpu.CompilerParams(
            dimension_semantics=("arbitrary",), vmem_limit_bytes=VMEM_LIMIT),
        name="out_router",
    )(attn2, rec2, x2, wo_bf, g, rw3, rb)


def _gather_rows(tok_ref, src_hbm, dst, sem, n_rows):
    def issue(r, carry):
        t = tok_ref[0, 0, r]
        pltpu.make_async_copy(src_hbm.at[pl.ds(t, 1)], dst.at[pl.ds(r, 1)], sem).start()
        return carry
    lax.fori_loop(0, n_rows, issue, 0, unroll=8)


def _expert_kernel(te_ref, na_ref, tok_ref, m_hbm, wgu_ref, bgu_ref, wd_ref, bd_ref, y_ref,
                   xbuf, sem, wgu_bf, wd_bf):
    i = pl.program_id(0)
    tm, d = xbuf.shape
    active = i < na_ref[0]

    @pl.when(active)
    def _():
        _gather_rows(tok_ref, m_hbm, xbuf, sem, tm)
        changed = jnp.logical_or(i == 0, te_ref[i] != te_ref[jnp.maximum(i - 1, 0)])

        @pl.when(changed)
        def _():
            wgu_bf[...] = wgu_ref[0].astype(BF16)
            wd_bf[...] = wd_ref[0].astype(BF16)

        pltpu.make_async_copy(m_hbm.at[pl.ds(0, tm)], xbuf, sem).wait()
        x = xbuf[...].astype(BF16)
        de = wd_bf.shape[0]
        gu = _dot(x, wgu_bf[...]) + bgu_ref[0]
        gate = jnp.minimum(gu[:, :de], SWIGLU_LIMIT)
        up = jnp.clip(gu[:, de:], -SWIGLU_LIMIT, SWIGLU_LIMIT)
        glu = gate * _sigmoid(SWIGLU_ALPHA * gate)
        y_ref[...] = _dot(((up + 1.0) * glu).astype(BF16), wd_bf[...]) + bd_ref[0]

    @pl.when(jnp.logical_not(active))
    def _():
        y_ref[...] = jnp.zeros_like(y_ref)


def _experts(tile_expert, n_active, row_token3, m2, w_gu, b_gu, w_down, b_down):
    t, d = m2.shape
    ne, _, de2 = w_gu.shape
    de = de2 // 2
    nt = row_token3.shape[0]
    tm = TM_EXPERT
    grid_spec = pltpu.PrefetchScalarGridSpec(
        num_scalar_prefetch=2,
        grid=(nt,),
        in_specs=[
            pl.BlockSpec((1, 1, tm), lambda i, te, na: (i, 0, 0), memory_space=pltpu.SMEM),
            pl.BlockSpec(memory_space=pl.ANY),
            pl.BlockSpec((1, d, de2), lambda i, te, na: (te[i], 0, 0)),
            pl.BlockSpec((1, 1, de2), lambda i, te, na: (te[i], 0, 0)),
            pl.BlockSpec((1, de, d), lambda i, te, na: (te[i], 0, 0)),
            pl.BlockSpec((1, 1, d), lambda i, te, na: (te[i], 0, 0)),
        ],
        out_specs=pl.BlockSpec((tm, d), lambda i, te, na: (i, 0)),
        scratch_shapes=[
            pltpu.VMEM((tm, d), F32),
            pltpu.SemaphoreType.DMA(()),
            pltpu.VMEM((d, de2), BF16),
            pltpu.VMEM((de, d), BF16),
        ],
    )
    return pl.pallas_call(
        _expert_kernel,
        grid_spec=grid_spec,
        out_shape=jax.ShapeDtypeStruct((nt * tm, d), F32),
        compiler_params=pltpu.CompilerParams(
            dimension_semantics=("arbitrary",), vmem_limit_bytes=VMEM_LIMIT),
        name="experts",
    )(tile_expert, n_active, row_token3, m2, w_gu, b_gu.reshape(ne, 1, de2), w_down,
      b_down.reshape(ne, 1, d))


def _combine_kernel(pos_ref, ys_hbm, wt_ref, h_ref, p_ref, wp_ref, wg_ref, o_ref, ybuf, sem):
    tm, d = h_ref.shape
    _gather_rows(pos_ref, ys_hbm, ybuf, sem, TOP_K * tm)
    pltpu.make_async_copy(ys_hbm.at[pl.ds(0, TOP_K * tm)], ybuf, sem).wait()
    wt = wt_ref[...]
    moe = wt[:, 0:1] * ybuf[0:tm, :]
    for kk in range(1, TOP_K):
        moe = moe + wt[:, kk:kk + 1] * ybuf[kk * tm:(kk + 1) * tm, :]
    h2 = h_ref[...] + moe
    gate = _sigmoid(_dot(h2.astype(BF16), wg_ref[...]))
    o_ref[...] = h2 + _dot(p_ref[...].astype(BF16), wp_ref[...]) * gate


def _combine_ple(pos3, ys, wt, h1, p2, wp_bf, wg_bf):
    t, d = h1.shape
    tm = TM_COMBINE
    pd = p2.shape[1]
    row = lambda i: (i, 0)
    const = lambda i: (0, 0)
    return pl.pallas_call(
        _combine_kernel,
        grid=(t // tm,),
        in_specs=[
            pl.BlockSpec((1, 1, TOP_K * tm), lambda i: (i, 0, 0), memory_space=pltpu.SMEM),
            pl.BlockSpec(memory_space=pl.ANY),
            pl.BlockSpec((tm, LANES), row),
            pl.BlockSpec((tm, d), row),
            pl.BlockSpec((tm, pd), row),
            pl.BlockSpec((pd, d), const),
            pl.BlockSpec((d, d), const),
        ],
        out_specs=pl.BlockSpec((tm, d), row),
        out_shape=jax.ShapeDtypeStruct((t, d), F32),
        scratch_shapes=[pltpu.VMEM((TOP_K * tm, d), F32), pltpu.SemaphoreType.DMA(())],
        compiler_params=pltpu.CompilerParams(
            dimension_semantics=("arbitrary",), vmem_limit_bytes=VMEM_LIMIT),
        name="combine_ple",
    )(pos3, ys, wt, h1, p2, wp_bf, wg_bf)


def _routing_plan(idx, tm):
    t = idx.shape[0]
    onehot = (idx[:, :, None] == jnp.arange(N_EXPERTS, dtype=jnp.int32)[None, None, :])
    onehot = onehot.astype(jnp.int32).sum(axis=1)
    cum = jnp.cumsum(onehot, axis=0)
    counts = cum[-1]
    rank = cum - onehot
    padded = ((counts + tm - 1) // tm) * tm
    ends = jnp.cumsum(padded)
    starts = ends - padded
    pos = starts[idx] + jnp.take_along_axis(rank, idx, axis=1)
    n_rows = TOP_K * t + N_EXPERTS * tm
    nt = n_rows // tm
    tile_start = jnp.arange(nt, dtype=jnp.int32) * tm
    tile_expert = (ends[None, :] <= tile_start[:, None]).astype(jnp.int32).sum(axis=1)
    tile_expert = jnp.minimum(tile_expert, N_EXPERTS - 1)
    n_active = (ends[-1] // tm).astype(jnp.int32).reshape(1)
    tok = jnp.repeat(jnp.arange(t, dtype=jnp.int32), TOP_K)
    row_token = jnp.zeros((n_rows,), jnp.int32).at[pos.reshape(-1)].set(tok)
    return pos.astype(jnp.int32), tile_expert, n_active, row_token.reshape(nt, 1, tm)


def kernel(x, p, positions, mix_norm_g, w_in, q_norm_g, k_norm_g, hgrn_lb_logits, hgrn_norm_g,
           w_out, ffn_norm_g, router_w, router_b, expert_w_gate_up, expert_b_gate_up,
           expert_w_down, expert_b_down, ple_proj, ple_gate):
    b, s, d = x.shape
    t = b * s
    depth = w_in.shape[0]
    assert depth == 1 and hgrn_lb_logits.shape[0] == 2
    assert s % (DILATED_BRANCHES[-1][1] * ATTN_BLOCK) == 0 and t % TM_PROJ == 0
    i = 0
    h = x.reshape(t, d)
    cos_t, sa_t, sb_t = _rope_tables(positions)
    tile_heads = lambda g: jnp.tile(g.reshape(1, HEAD_DIM), (1, ATTN_WIDTH // HEAD_DIM))

    q, k, v, zh = _in_proj(h, mix_norm_g[i].reshape(1, d), w_in[i].astype(BF16),
                           tile_heads(q_norm_g[i]), tile_heads(k_norm_g[i]), cos_t, sa_t, sb_t)
    attn = _attention(q.reshape(b, s, -1), k.reshape(b, s, -1), v.reshape(b, s, -1))
    rec = _hgrn(zh.reshape(b, s, -1), hgrn_lb_logits, hgrn_norm_g[i].reshape(1, HGRN_EXPAND))

    rw3 = jnp.stack(_split3(router_w[i]))
    h1, m, idx128, wt128 = _out_router(
        attn.reshape(t, -1), rec.reshape(t, -1), h, w_out[i].astype(BF16),
        ffn_norm_g[i].reshape(1, d), rw3, router_b[i].reshape(1, N_EXPERTS))

    pos, tile_expert, n_active, row_token3 = _routing_plan(idx128[:, :TOP_K], TM_EXPERT)
    ys = _experts(tile_expert, n_active, row_token3, m, expert_w_gate_up[i], expert_b_gate_up[i],
                  expert_w_down[i], expert_b_down[i])

    pos3 = pos.reshape(t // TM_COMBINE, TM_COMBINE, TOP_K).transpose(0, 2, 1)
    pos3 = pos3.reshape(t // TM_COMBINE, 1, TOP_K * TM_COMBINE)
    out = _combine_ple(pos3, ys, wt128, h1, p[i].reshape(t, -1), ple_proj[i].astype(BF16),
                       ple_gate[i].astype(BF16))
    return out.reshape(b, s, d)
```
